```python
import jax, jax.numpy as jnp
from jax import lax
import numpy as np

D_MODEL = 2048
BATCH = 1
SEQ = 8192
DEPTH = 4

N_MIXERS = 3
HEAD_DIM = 128
N_HEADS = D_MODEL // HEAD_DIM
FFN_DIM = -(-(8 * D_MODEL // 3) // 256) * 256
CONV_WIDTH = 3
Q_BLOCK = 128
NORM_EPS = 1e-6
N_FOX = (DEPTH + 2) // 3
N_CONV = (DEPTH + 1) // 3
N_SB = DEPTH // 3

kernel_name = "hybrid_fox_shortconv_stickbreaking_macaron"


def rms_norm(x, g):
    xf = x.astype(jnp.float32)
    y = xf * lax.rsqrt(jnp.mean(xf * xf, axis=-1, keepdims=True) + NORM_EPS)
    return (y * g.astype(jnp.float32)).astype(x.dtype)


def swiglu(x, w_gate, w_up, w_down):
    return (jax.nn.silu(x @ w_gate) * (x @ w_up)) @ w_down


def split_heads(t):
    b, s, _ = t.shape
    return t.reshape(b, s, N_HEADS, HEAD_DIM).transpose(0, 2, 1, 3)


def to_query_blocks(t):
    b, h, s, d = t.shape
    return jnp.moveaxis(t.reshape(b, h, s // Q_BLOCK, Q_BLOCK, d), 2, 0)


def from_query_blocks(o):
    nb, b, h, qb, d = o.shape
    o = jnp.moveaxis(o, 0, 2).reshape(b, h, nb * qb, d)
    return o.transpose(0, 2, 1, 3).reshape(b, nb * qb, h * d)


def fox_mixer(h, w_in, b_f, q_gain, k_gain, w_out):
    b, s, _ = h.shape
    proj = h @ w_in
    q, k, v, f_logit = jnp.split(proj, [D_MODEL, 2 * D_MODEL, 3 * D_MODEL], axis=-1)
    q = rms_norm(q.reshape(b, s, N_HEADS, HEAD_DIM), q_gain)
    k = rms_norm(k.reshape(b, s, N_HEADS, HEAD_DIM), k_gain)
    log_f = jax.nn.log_sigmoid((f_logit + b_f).astype(jnp.float32))
    c = jnp.cumsum(log_f, axis=1).transpose(0, 2, 1)
    qh = q.transpose(0, 2, 1, 3).astype(jnp.float32)
    kh = k.transpose(0, 2, 1, 3).astype(jnp.float32)
    vh = split_heads(v)
    nb = s // Q_BLOCK
    q_blocks = to_query_blocks(qh)
    c_blocks = jnp.moveaxis(c.reshape(b, N_HEADS, nb, Q_BLOCK), 2, 0)
    key_pos = jnp.arange(s)
    scale = HEAD_DIM ** -0.5

    def one_block(args):
        qb, cb, i = args
        q_pos = i * Q_BLOCK + jnp.arange(Q_BLOCK)
        logits = jnp.einsum('bhqd,bhkd->bhqk', qb, kh) * scale
        logits = logits + cb[..., None] - c[:, :, None, :]
        logits = jnp.where(key_pos[None, :] <= q_pos[:, None], logits, -jnp.inf)
        p = jax.nn.softmax(logits, axis=-1)
        return jnp.einsum('bhqk,bhkd->bhqd', p.astype(vh.dtype), vh)

    o = lax.map(one_block, (q_blocks, c_blocks, jnp.arange(nb)))
    return from_query_blocks(o) @ w_out


def conv_mixer(h, w_in, conv_kernel, w_out):
    proj = h @ w_in
    b_gate, c_gate, u = jnp.split(proj, 3, axis=-1)
    inner = c_gate * u
    conv = lax.conv_general_dilated(
        inner, conv_kernel[:, None, :].astype(inner.dtype),
        window_strides=(1,), padding=[(CONV_WIDTH - 1, 0)],
        dimension_numbers=('NWC', 'WIO', 'NWC'),
        feature_group_count=D_MODEL)
    return (b_gate * conv) @ w_out


def stick_breaking_mixer(h, w_in, w_out):
    b, s, _ = h.shape
    proj = h @ w_in
    q, k, v = jnp.split(proj, 3, axis=-1)
    qh = split_heads(q).astype(jnp.float32)
    kh = split_heads(k).astype(jnp.float32)
    vh = split_heads(v)
    nb = s // Q_BLOCK
    q_blocks = to_query_blocks(qh)
    key_pos = jnp.arange(s)
    scale = HEAD_DIM ** -0.5

    def one_block(args):
        qb, i = args
        q_pos = i * Q_BLOCK + jnp.arange(Q_BLOCK)
        z = jnp.einsum('bhqd,bhkd->bhqk', qb, kh) * scale
        strict = key_pos[None, :] < q_pos[:, None]
        log_beta = jax.nn.log_sigmoid(z)
        log_one_minus = jnp.where(strict, jax.nn.log_sigmoid(-z), 0.0)
        between = lax.cumsum(log_one_minus, axis=3, reverse=True) - log_one_minus
        a = jnp.where(strict, jnp.exp(log_beta + between), 0.0)
        return jnp.einsum('bhqk,bhkd->bhqd', a.astype(vh.dtype), vh)

    o = lax.map(one_block, (q_blocks, jnp.arange(nb)))
    return from_query_blocks(o) @ w_out


def setup_inputs(seed: int = 0) -> dict:
    key = jax.random.key(seed)
    ks = jax.random.split(key, 24)
    f32 = jnp.float32
    d, f, hd, h = D_MODEL, FFN_DIM, HEAD_DIM, N_HEADS

    def nrm(k, shape, fan_in):
        return jax.random.normal(k, shape, f32) * (fan_in ** -0.5)

    def gain(k, shape):
        return 1.0 + 0.05 * jax.random.normal(k, shape, f32)

    return {
        "x": jax.random.normal(ks[0], (BATCH, SEQ, d), f32),
        "norm_ffn1": gain(ks[1], (DEPTH, d)),
        "norm_mix": gain(ks[2], (DEPTH, d)),
        "norm_ffn2": gain(ks[3], (DEPTH, d)),
        "ffn1_w_gate": nrm(ks[4], (DEPTH, d, f), d),
        "ffn1_w_up": nrm(ks[5], (DEPTH, d, f), d),
        "ffn1_w_down": nrm(ks[6], (DEPTH, f, d), f),
        "ffn2_w_gate": nrm(ks[7], (DEPTH, d, f), d),
        "ffn2_w_up": nrm(ks[8], (DEPTH, d, f), d),
        "ffn2_w_down": nrm(ks[9], (DEPTH, f, d), f),
        "fox_w_in": nrm(ks[10], (N_FOX, d, 3 * d + h), d),
        "fox_b_f": jax.random.uniform(ks[11], (N_FOX, h), f32, 1.0, 4.0),
        "fox_q_gain": gain(ks[12], (N_FOX, hd)),
        "fox_k_gain": gain(ks[13], (N_FOX, hd)),
        "fox_w_out": nrm(ks[14], (N_FOX, d, d), d),
        "conv_w_in": nrm(ks[15], (N_CONV, d, 3 * d), d),
        "conv_kernel": nrm(ks[16], (N_CONV, CONV_WIDTH, d), CONV_WIDTH),
        "conv_w_out": nrm(ks[17], (N_CONV, d, d), d),
        "sb_w_in": nrm(ks[18], (N_SB, d, 3 * d), d),
        "sb_w_out": nrm(ks[19], (N_SB, d, d), d),
    }


def reference(x, norm_ffn1, norm_mix, norm_ffn2,
              ffn1_w_gate, ffn1_w_up, ffn1_w_down,
              ffn2_w_gate, ffn2_w_up, ffn2_w_down,
              fox_w_in, fox_b_f, fox_q_gain, fox_k_gain, fox_w_out,
              conv_w_in, conv_kernel, conv_w_out,
              sb_w_in, sb_w_out):
    for i in range(DEPTH):
        x = x + 0.5 * swiglu(rms_norm(x, norm_ffn1[i]), ffn1_w_gate[i], ffn1_w_up[i], ffn1_w_down[i])
        hn = rms_norm(x, norm_mix[i])
        kind, j = i % N_MIXERS, i // N_MIXERS
        if kind == 0:
            m = fox_mixer(hn, fox_w_in[j], fox_b_f[j], fox_q_gain[j], fox_k_gain[j], fox_w_out[j])
        elif kind == 1:
            m = conv_mixer(hn, conv_w_in[j], conv_kernel[j], conv_w_out[j])
        else:
            m = stick_breaking_mixer(hn, sb_w_in[j], sb_w_out[j])
        x = x + m
        x = x + 0.5 * swiglu(rms_norm(x, norm_ffn2[i]), ffn2_w_gate[i], ffn2_w_up[i], ffn2_w_down[i])
    return x
```

```python
import functools

import jax
import jax.numpy as jnp
from jax import lax
from jax.experimental import pallas as pl
from jax.experimental.pallas import tpu as pltpu

F32 = jnp.float32
BF16 = jnp.bfloat16
NORM_EPS = 1e-6
HEAD_DIM = 128
CONV_WIDTH = 3
MASK_VALUE = -1e30
MIB = 1024 * 1024


def _params(semantics, vmem_mib):
    return pltpu.CompilerParams(dimension_semantics=semantics, vmem_limit_bytes=vmem_mib * MIB)


def _tile(n, pref):
    t = min(n, pref)
    assert n % t == 0, (n, pref)
    return t


def _rms(x, g):
    ms = jnp.mean(x * x, axis=-1, keepdims=True)
    return x * lax.rsqrt(ms + NORM_EPS) * g


def _log_sigmoid(x):
    return jnp.minimum(x, 0.0) - jnp.log1p(jnp.exp(-jnp.abs(x)))


def _ffn_kernel(x_ref, g_ref, wg_ref, wu_ref, wd_ref, o_ref, xn_ref):
    f = pl.program_id(1)

    @pl.when(f == 0)
    def _():
        xn_ref[...] = _rms(x_ref[...], g_ref[...]).astype(BF16)
        o_ref[...] = jnp.zeros_like(o_ref)

    xn = xn_ref[...]
    gate = jnp.dot(xn, wg_ref[...], preferred_element_type=F32)
    up = jnp.dot(xn, wu_ref[...], preferred_element_type=F32)
    h = (gate * jax.nn.sigmoid(gate) * up).astype(BF16)
    o_ref[...] += jnp.dot(h, wd_ref[...], preferred_element_type=F32)

    @pl.when(f == pl.num_programs(1) - 1)
    def _():
        o_ref[...] = x_ref[...] + 0.5 * o_ref[...]


def _ffn(x, gain, w_gate, w_up, w_down, layer):
    s, d = x.shape
    f = w_gate.shape[-1]
    tm, tf = _tile(s, 512), _tile(f, 512)
    return pl.pallas_call(
        _ffn_kernel,
        grid=(s // tm, f // tf),
        in_specs=[
            pl.BlockSpec((tm, d), lambda i, j: (i, 0)),
            pl.BlockSpec((None, 1, d), lambda i, j: (layer, 0, 0)),
            pl.BlockSpec((None, d, tf), lambda i, j: (layer, 0, j)),
            pl.BlockSpec((None, d, tf), lambda i, j: (layer, 0, j)),
            pl.BlockSpec((None, tf, d), lambda i, j: (layer, j, 0)),
        ],
        out_specs=pl.BlockSpec((tm, d), lambda i, j: (i, 0)),
        out_shape=jax.ShapeDtypeStruct((s, d), F32),
        scratch_shapes=[pltpu.VMEM((tm, d), BF16)],
        compiler_params=_params(("parallel", "arbitrary"), 48),
        name="ffn",
    )(x, gain, w_gate, w_up, w_down)


def _qkv_kernel(*refs, blocks_per_part, qk_norm, with_gate, scale):
    if with_gate:
        x_ref, g_ref, w_ref, qg_ref, kg_ref, wf_ref, o_ref, f_ref, xn_ref = refs
    else:
        x_ref, g_ref, w_ref, qg_ref, kg_ref, o_ref, xn_ref = refs
    n = pl.program_id(1)

    @pl.when(n == 0)
    def _():
        xn = _rms(x_ref[...], g_ref[...]).astype(BF16)
        xn_ref[...] = xn
        if with_gate:
            f_ref[...] = jnp.dot(xn, wf_ref[...], preferred_element_type=F32)

    acc = jnp.dot(xn_ref[...], w_ref[...], preferred_element_type=F32)
    part = n // blocks_per_part
    heads = acc.shape[1] // HEAD_DIM

    def head_norm(gain_ref, mult):
        for hh in range(heads):
            blk = acc[:, hh * HEAD_DIM:(hh + 1) * HEAD_DIM]
            ms = jnp.mean(blk * blk, axis=-1, keepdims=True)
            y = blk * lax.rsqrt(ms + NORM_EPS) * gain_ref[...]
            o_ref[:, hh * HEAD_DIM:(hh + 1) * HEAD_DIM] = (y * mult).astype(BF16)

    @pl.when(part == 0)
    def _():
        if qk_norm:
            head_norm(qg_ref, scale)
        else:
            o_ref[...] = (acc * scale).astype(BF16)

    @pl.when(part == 1)
    def _():
        if qk_norm:
            head_norm(kg_ref, 1.0)
        else:
            o_ref[...] = acc.astype(BF16)

    @pl.when(part == 2)
    def _():
        o_ref[...] = acc.astype(BF16)


def _qkv_proj(x, gain, w, q_gain, k_gain, w_f, layer, *, qk_norm):
    s, d = x.shape
    n_out = w.shape[1]
    with_gate = w_f is not None
    tm, tn = _tile(s, 512), _tile(d, 512)
    kern = functools.partial(_qkv_kernel, blocks_per_part=d // tn, qk_norm=qk_norm, with_gate=with_gate,
                             scale=HEAD_DIM ** -0.5)
    in_specs = [
        pl.BlockSpec((tm, d), lambda i, j: (i, 0)),
        pl.BlockSpec((None, 1, d), lambda i, j: (layer, 0, 0)),
        pl.BlockSpec((d, tn), lambda i, j: (0, j)),
        pl.BlockSpec((1, HEAD_DIM), lambda i, j: (0, 0)),
        pl.BlockSpec((1, HEAD_DIM), lambda i, j: (0, 0)),
    ]
    out_specs = [pl.BlockSpec((tm, tn), lambda i, j: (i, j))]
    out_shape = [jax.ShapeDtypeStruct((s, n_out), BF16)]
    args = [x, gain, w, q_gain, k_gain]
    if with_gate:
        in_specs.append(pl.BlockSpec((d, HEAD_DIM), lambda i, j: (0, 0)))
        out_specs.append(pl.BlockSpec((tm, HEAD_DIM), lambda i, j: (i, 0)))
        out_shape.append(jax.ShapeDtypeStruct((s, HEAD_DIM), F32))
        args.append(w_f)
    return pl.pallas_call(
        kern,
        grid=(s // tm, n_out // tn),
        in_specs=in_specs,
        out_specs=out_specs,
        out_shape=out_shape,
        scratch_shapes=[pltpu.VMEM((tm, d), BF16)],
        compiler_params=_params(("parallel", "arbitrary"), 40),
        name="qkv_proj",
    )(*args)


def _fox_gate_kernel(f_ref, b_ref, c_ref):
    c = _log_sigmoid(f_ref[...] + b_ref[...])
    rows = c.shape[0]
    row = lax.broadcasted_iota(jnp.int32, c.shape, 0)
    shift = 1
    while shift < rows:
        c = c + jnp.where(row >= shift, pltpu.roll(c, shift, axis=0), 0.0)
        shift *= 2
    c_ref[...] = c


def _fox_gate(f_logit, b_f):
    s = f_logit.shape[0]
    return pl.pallas_call(
        _fox_gate_kernel,
        grid=(1,),
        in_specs=[
            pl.BlockSpec((s, HEAD_DIM), lambda i: (0, 0)),
            pl.BlockSpec((1, HEAD_DIM), lambda i: (0, 0)),
        ],
        out_specs=pl.BlockSpec((s, HEAD_DIM), lambda i: (0, 0)),
        out_shape=jax.ShapeDtypeStruct((s, HEAD_DIM), F32),
        compiler_params=_params(("arbitrary",), 48),
        name="fox_gate",
    )(f_logit, b_f)


def _fox_attn_kernel(q_ref, k_ref, v_ref, c_ref, o_ref, m_ref, l_ref, acc_ref, *, tq, tk):
    i = pl.program_id(1)
    ratio = tq // tk
    q = q_ref[...]
    m_ref[...] = jnp.full_like(m_ref, MASK_VALUE)
    l_ref[...] = jnp.zeros_like(l_ref)
    acc_ref[...] = jnp.zeros_like(acc_ref)
    c0 = jnp.max(c_ref[i * ratio], axis=-1, keepdims=True)

    def tile(j, masked):
        start = pl.multiple_of(j * tk, tk)
        k = k_ref[pl.ds(start, tk), :]
        v = v_ref[pl.ds(start, tk), :]
        s = lax.dot_general(q, k, (((1,), (1,)), ((), ())), preferred_element_type=F32)
        s = s + (c0 - c_ref[j])
        if masked:
            qpos = i * tq + lax.broadcasted_iota(jnp.int32, s.shape, 0)
            kpos = j * tk + lax.broadcasted_iota(jnp.int32, s.shape, 1)
            s = jnp.where(kpos <= qpos, s, MASK_VALUE)
        m_prev = m_ref[...]
        m_next = jnp.maximum(m_prev, jnp.max(s, axis=1, keepdims=True))
        p = jnp.exp(s - m_next[:, :1])
        alpha = jnp.exp(m_prev - m_next)
        l_ref[...] = alpha * l_ref[...] + jnp.sum(p, axis=1, keepdims=True)
        m_ref[...] = m_next
        acc_ref[...] = alpha * acc_ref[...] + jnp.dot(p.astype(BF16), v, preferred_element_type=F32)

    def full_tile(j, carry):
        tile(j, False)
        return carry

    lax.fori_loop(0, i * ratio, full_tile, 0)
    for dd in range(ratio):
        tile(i * ratio + dd, True)
    o_ref[...] = (acc_ref[...] / l_ref[...]).astype(BF16)


def _fox_attn(qkv, c_blocks, *, tq, tk):
    s = qkv.shape[0]
    d = qkv.shape[1] // 3
    h = d // HEAD_DIM
    nk = s // tk
    kern = functools.partial(_fox_attn_kernel, tq=tq, tk=tk)
    return pl.pallas_call(
        kern,
        grid=(h, s // tq),
        in_specs=[
            pl.BlockSpec((tq, HEAD_DIM), lambda hh, i: (i, hh)),
            pl.BlockSpec((s, HEAD_DIM), lambda hh, i: (0, h + hh)),
            pl.BlockSpec((s, HEAD_DIM), lambda hh, i: (0, 2 * h + hh)),
            pl.BlockSpec((None, nk, 1, tk), lambda hh, i: (hh, 0, 0, 0)),
        ],
        out_specs=pl.BlockSpec((tq, HEAD_DIM), lambda hh, i: (i, hh)),
        out_shape=jax.ShapeDtypeStruct((s, d), BF16),
        scratch_shapes=[
            pltpu.VMEM((tq, HEAD_DIM), F32),
            pltpu.VMEM((tq, HEAD_DIM), F32),
            pltpu.VMEM((tq, HEAD_DIM), F32),
        ],
        compiler_params=_params(("parallel", "arbitrary"), 32),
        name="fox_attn",
    )(qkv, qkv, qkv, c_blocks)


def _sb_attn_kernel(qt_ref, k_ref, vt_ref, ot_ref, acc_ref, run_ref, *, tq, tk):
    i = pl.program_id(1)
    ratio = tq // tk
    na = tk // 8
    qt = qt_ref[...]
    acc_ref[...] = jnp.zeros_like(acc_ref)
    run_ref[...] = jnp.zeros_like(run_ref)
    sub = lax.broadcasted_iota(jnp.int32, (8, tq), 0)

    def tile(j, masked):
        z = jnp.dot(k_ref[j], qt, preferred_element_type=F32)
        log_beta = _log_sigmoid(z)
        log_rest = log_beta - z
        if masked:
            r = lax.broadcasted_iota(jnp.int32, z.shape, 0)
            kpos = j * tk + (r & 7) * na + (r >> 3)
            qpos = i * tq + lax.broadcasted_iota(jnp.int32, z.shape, 1)
            valid = kpos < qpos
            log_rest = jnp.where(valid, log_rest, 0.0)
        suffix = jnp.zeros((8, tq), F32)
        later = [None] * na
        for a in reversed(range(na)):
            later[a] = suffix
            suffix = suffix + log_rest[a * 8:(a + 1) * 8, :]
        offset = jnp.zeros((8, tq), F32)
        for kk in range(1, 8):
            offset = offset + jnp.where(sub + kk < 8, pltpu.roll(suffix, 8 - kk, axis=0), 0.0)
        run = run_ref[...]
        base = offset + run
        between = jnp.concatenate([later[a] + base for a in range(na)], axis=0)
        p = jnp.exp(log_beta + between)
        if masked:
            p = jnp.where(valid, p, 0.0)
        acc_ref[...] += jnp.dot(vt_ref[j], p.astype(BF16), preferred_element_type=F32)
        run_ref[...] = run + jnp.sum(suffix, axis=0, keepdims=True)

    last = (i + 1) * ratio - 1
    for dd in range(ratio):
        tile(last - dd, True)

    def full_tile(t, carry):
        tile(i * ratio - 1 - t, False)
        return carry

    lax.fori_loop(0, i * ratio, full_tile, 0)
    ot_ref[...] = acc_ref[...].astype(BF16)


def _sb_attn(qt, k_perm, vt_perm, *, tq, tk):
    h, _, s = qt.shape
    nk = s // tk
    kern = functools.partial(_sb_attn_kernel, tq=tq, tk=tk)
    return pl.pallas_call(
        kern,
        grid=(h, s // tq),
        in_specs=[
            pl.BlockSpec((None, HEAD_DIM, tq), lambda hh, i: (hh, 0, i)),
            pl.BlockSpec((None, nk, tk, HEAD_DIM), lambda hh, i: (hh, 0, 0, 0)),
            pl.BlockSpec((None, nk, HEAD_DIM, tk), lambda hh, i: (hh, 0, 0, 0)),
        ],
        out_specs=pl.BlockSpec((None, HEAD_DIM, tq), lambda hh, i: (hh, 0, i)),
        out_shape=jax.ShapeDtypeStruct((h, HEAD_DIM, s), BF16),
        scratch_shapes=[pltpu.VMEM((HEAD_DIM, tq), F32), pltpu.VMEM((1, tq), F32)],
        compiler_params=_params(("parallel", "arbitrary"), 32),
        name="sb_attn",
    )(qt, k_perm, vt_perm)


def _out_proj_kernel(x_ref, a_ref, w_ref, o_ref):
    o_ref[...] = x_ref[...] + jnp.dot(a_ref[...], w_ref[...], preferred_element_type=F32)


def _out_proj(x, a, w):
    s, d = x.shape
    tm = _tile(s, 512)
    return pl.pallas_call(
        _out_proj_kernel,
        grid=(s // tm,),
        in_specs=[
            pl.BlockSpec((tm, d), lambda i: (i, 0)),
            pl.BlockSpec((tm, d), lambda i: (i, 0)),
            pl.BlockSpec((d, d), lambda i: (0, 0)),
        ],
        out_specs=pl.BlockSpec((tm, d), lambda i: (i, 0)),
        out_shape=jax.ShapeDtypeStruct((s, d), F32),
        compiler_params=_params(("parallel",), 48),
        name="out_proj",
    )(x, a, w)


def _conv_in_kernel(x_ref, g_ref, wb_ref, wc_ref, wu_ref, gate_ref, inner_ref, xn_ref):
    @pl.when(pl.program_id(1) == 0)
    def _():
        xn_ref[...] = _rms(x_ref[...], g_ref[...]).astype(BF16)

    xn = xn_ref[...]
    gate_ref[...] = jnp.dot(xn, wb_ref[...], preferred_element_type=F32).astype(BF16)
    c = jnp.dot(xn, wc_ref[...], preferred_element_type=F32)
    u = jnp.dot(xn, wu_ref[...], preferred_element_type=F32)
    inner_ref[...] = (c * u).astype(BF16)


def _conv_in(x, gain, w, layer):
    s, d = x.shape
    tm, tn = _tile(s, 512), _tile(d, 512)
    nb = d // tn
    out = jax.ShapeDtypeStruct((s, d), BF16)
    return pl.pallas_call(
        _conv_in_kernel,
        grid=(s // tm, nb),
        in_specs=[
            pl.BlockSpec((tm, d), lambda i, j: (i, 0)),
            pl.BlockSpec((None, 1, d), lambda i, j: (layer, 0, 0)),
            pl.BlockSpec((d, tn), lambda i, j: (0, j)),
            pl.BlockSpec((d, tn), lambda i, j: (0, nb + j)),
            pl.BlockSpec((d, tn), lambda i, j: (0, 2 * nb + j)),
        ],
        out_specs=[pl.BlockSpec((tm, tn), lambda i, j: (i, j)), pl.BlockSpec((tm, tn), lambda i, j: (i, j))],
        out_shape=[out, out],
        scratch_shapes=[pltpu.VMEM((tm, d), BF16)],
        compiler_params=_params(("parallel", "arbitrary"), 40),
        name="conv_in",
    )(x, gain, w, w, w)


def _conv_out_kernel(x_ref, gate_ref, inner_ref, halo_ref, ck_ref, w_ref, o_ref):
    i = pl.program_id(0)
    inner = inner_ref[...].astype(F32)
    halo = halo_ref[...].astype(F32) * (i > 0).astype(F32)
    rows = halo.shape[0]
    prev1 = halo[rows - 1:rows, :]
    prev2 = halo[rows - 2:rows - 1, :]
    row = lax.broadcasted_iota(jnp.int32, inner.shape, 0)
    back1 = jnp.where(row == 0, prev1, pltpu.roll(inner, 1, axis=0))
    back2 = jnp.where(row == 0, prev2, jnp.where(row == 1, prev1, pltpu.roll(inner, 2, axis=0)))
    conv = ck_ref[2:3, :] * inner + ck_ref[1:2, :] * back1 + ck_ref[0:1, :] * back2
    y = (gate_ref[...].astype(F32) * conv).astype(BF16)
    o_ref[...] = x_ref[...] + jnp.dot(y, w_ref[...], preferred_element_type=F32)


def _conv_out(x, gate, inner, conv_kernel, w):
    s, d = x.shape
    tm = _tile(s, 512)
    halo_rows = 16
    per = tm // halo_rows
    return pl.pallas_call(
        _conv_out_kernel,
        grid=(s // tm,),
        in_specs=[
            pl.BlockSpec((tm, d), lambda i: (i, 0)),
            pl.BlockSpec((tm, d), lambda i: (i, 0)),
            pl.BlockSpec((tm, d), lambda i: (i, 0)),
            pl.BlockSpec((halo_rows, d), lambda i: (jnp.maximum(i * per - 1, 0), 0)),
            pl.BlockSpec((CONV_WIDTH, d), lambda i: (0, 0)),
            pl.BlockSpec((d, d), lambda i: (0, 0)),
        ],
        out_specs=pl.BlockSpec((tm, d), lambda i: (i, 0)),
        out_shape=jax.ShapeDtypeStruct((s, d), F32),
        compiler_params=_params(("parallel",), 56),
        name="conv_out",
    )(x, gate, inner, inner, conv_kernel, w)


def _fox_mixer(x, gain, layer, w_in, b_f, q_gain, k_gain, w_out):
    s, d = x.shape
    h = d // HEAD_DIM
    tq = tk = _tile(s, 512)
    w_qkv = w_in[:, :3 * d].astype(BF16)
    w_f = jnp.pad(w_in[:, 3 * d:], ((0, 0), (0, HEAD_DIM - h))).astype(BF16)
    b_pad = jnp.pad(b_f, (0, HEAD_DIM - h)).reshape(1, HEAD_DIM)
    qkv, f_logit = _qkv_proj(x, gain, w_qkv, q_gain.reshape(1, HEAD_DIM), k_gain.reshape(1, HEAD_DIM), w_f, layer,
                             qk_norm=True)
    c = _fox_gate(f_logit, b_pad)
    c_blocks = c[:, :h].T.reshape(h, s // tk, 1, tk)
    o = _fox_attn(qkv, c_blocks, tq=tq, tk=tk)
    return _out_proj(x, o, w_out.astype(BF16))


def _sb_mixer(x, gain, layer, w_in, w_out):
    s, d = x.shape
    h = d // HEAD_DIM
    tq, tk = _tile(s, 512), _tile(s, 256)
    ones = jnp.ones((1, HEAD_DIM), F32)
    (qkv,) = _qkv_proj(x, gain, w_in.astype(BF16), ones, ones, None, layer, qk_norm=False)
    q, k, v = qkv[:, :d], qkv[:, d:2 * d], qkv[:, 2 * d:]
    nk = s // tk
    qt = q.reshape(s, h, HEAD_DIM).transpose(1, 2, 0)
    k_perm = k.reshape(nk, 8, tk // 8, h, HEAD_DIM).transpose(3, 0, 2, 1, 4).reshape(h, nk, tk, HEAD_DIM)
    vt_perm = v.reshape(nk, 8, tk // 8, h, HEAD_DIM).transpose(3, 0, 4, 2, 1).reshape(h, nk, HEAD_DIM, tk)
    ot = _sb_attn(qt, k_perm, vt_perm, tq=tq, tk=tk)
    o = ot.transpose(2, 0, 1).reshape(s, d)
    return _out_proj(x, o, w_out.astype(BF16))


def _conv_mixer(x, gain, layer, w_in, conv_kernel, w_out):
    gate, inner = _conv_in(x, gain, w_in.astype(BF16), layer)
    return _conv_out(x, gate, inner, conv_kernel, w_out.astype(BF16))


def kernel(x, norm_ffn1, norm_mix, norm_ffn2, ffn1_w_gate, ffn1_w_up, ffn1_w_down, ffn2_w_gate, ffn2_w_up, ffn2_w_down, fox_w_in, fox_b_f, fox_q_gain, fox_k_gain, fox_w_out, conv_w_in, conv_kernel, conv_w_out, sb_w_in, sb_w_out):
    b, s, d = x.shape
    depth = norm_ffn1.shape[0]
    outs = []
    ffn1 = [w.astype(BF16) for w in (ffn1_w_gate, ffn1_w_up, ffn1_w_down)]
    ffn2 = [w.astype(BF16) for w in (ffn2_w_gate, ffn2_w_up, ffn2_w_down)]
    g_ffn1, g_mix, g_ffn2 = (g.reshape(depth, 1, d) for g in (norm_ffn1, norm_mix, norm_ffn2))
    for bi in range(b):
        y = x[bi]
        for i in range(depth):
            y = _ffn(y, g_ffn1, *ffn1, i)
            kind, j = i % 3, i // 3
            if kind == 0:
                y = _fox_mixer(y, g_mix, i, fox_w_in[j], fox_b_f[j], fox_q_gain[j], fox_k_gain[j], fox_w_out[j])
            elif kind == 1:
                y = _conv_mixer(y, g_mix, i, conv_w_in[j], conv_kernel[j], conv_w_out[j])
            else:
                y = _sb_mixer(y, g_mix, i, sb_w_in[j], sb_w_out[j])
            y = _ffn(y, g_ffn2, *ffn2, i)
        outs.append(y)
    return jnp.stack(outs, axis=0)
```
